```python
import jax
import jax.numpy as jnp
from jax import lax
import numpy as np

D_MODEL = 1024
BATCH = 8
SEQ = 2048
DEPTH = 1

GRID_W = 64
CTX_LEN = 256

NA_HEADS = 8
NA_HEAD_DIM = 64
NA_WIDTH = NA_HEADS * NA_HEAD_DIM
NA_KH = 8
NA_KW = 16
NA_QB_W = 16
NA_KB_W = NA_QB_W + NA_KW
NA_NCB = GRID_W // NA_QB_W

HG_HEADS = 4
HG_DK = 128
HG_WIDTH = HG_HEADS * HG_DK
HG_CHUNK = 64

MIX_WIDTH = NA_WIDTH + HG_WIDTH
IN_COLS = 3 * NA_WIDTH + 5 * HG_WIDTH

PEER_HEADS = 8
PEER_NKEYS = 128
PEER_EXPERTS = PEER_NKEYS * PEER_NKEYS
PEER_TOPK = 16
PEER_DQ = 256
PEER_BLOCK = 128

N_MOD = 6
EPS = 1e-6

kernel_name = "hybrid_natten_hgrn2_peer_dit_layer"


def rms_norm(x, g):
    xf = x.astype(jnp.float32)
    y = xf * lax.rsqrt(jnp.mean(xf * xf, axis=-1, keepdims=True) + EPS)
    return (y * g.astype(jnp.float32)).astype(x.dtype)


def modulate(h, shift, scale):
    return h * (1 + scale) + shift


def neighbourhood_attention(q, k, v, k_ctx, v_ctx, rpb):
    B, S, H, Dh = q.shape
    rows = S // GRID_W
    kh = min(NA_KH, rows)
    scale = Dh ** -0.5
    qg = q.reshape(B, rows, GRID_W, H, Dh)
    kg = k.reshape(B, rows, GRID_W, H, Dh)
    vg = v.reshape(B, rows, GRID_W, H, Dh)
    qcol = np.arange(GRID_W).reshape(NA_NCB, NA_QB_W)
    kc0 = np.clip(qcol[:, 0] - NA_KW // 2, 0, GRID_W - NA_KB_W)
    kcol = kc0[:, None] + np.arange(NA_KB_W)[None, :]
    cstart = np.clip(qcol - NA_KW // 2, 0, GRID_W - NA_KW)
    col_ok = (kcol[:, None, :] >= cstart[..., None]) & (kcol[:, None, :] < cstart[..., None] + NA_KW)
    dc_idx = np.clip(kcol[:, None, :] - qcol[..., None] + NA_KW - 1, 0, 2 * NA_KW - 2)
    n_win = kh * NA_KB_W

    def row_block(r):
        rs = jnp.clip(r - kh // 2, 0, rows - kh)
        q_r = lax.dynamic_index_in_dim(qg, r, axis=1, keepdims=False).reshape(B, NA_NCB, NA_QB_W, H, Dh)
        k_rows = lax.dynamic_slice_in_dim(kg, rs, kh, axis=1)
        v_rows = lax.dynamic_slice_in_dim(vg, rs, kh, axis=1)
        kb = k_rows[:, :, kcol]
        vb = v_rows[:, :, kcol]
        dr_idx = rs + jnp.arange(kh) - r + NA_KH - 1
        bias = rpb[:, dr_idx[None, None, :, None], dc_idx[:, :, None, :]]
        s_win = jnp.einsum('bcqhd,bkcjhd->bhcqkj', q_r, kb).astype(jnp.float32) * scale
        s_win = jnp.where(col_ok[None, None, :, :, None, :], s_win + bias.astype(jnp.float32)[None], -jnp.inf)
        s_ctx = jnp.einsum('bcqhd,blhd->bhcql', q_r, k_ctx).astype(jnp.float32) * scale
        p = jax.nn.softmax(jnp.concatenate([s_win.reshape(B, H, NA_NCB, NA_QB_W, n_win), s_ctx], axis=-1), axis=-1)
        p = p.astype(v.dtype)
        p_win = p[..., :n_win].reshape(B, H, NA_NCB, NA_QB_W, kh, NA_KB_W)
        o = (jnp.einsum('bhcqkj,bkcjhd->bcqhd', p_win, vb)
             + jnp.einsum('bhcql,blhd->bcqhd', p[..., n_win:], v_ctx))
        return o.reshape(B, GRID_W, H, Dh)

    out = lax.map(row_block, jnp.arange(rows))
    return jnp.moveaxis(out, 0, 1).reshape(B, S, H * Dh)


def context_attention(q, k, v):
    B, L, H, Dh = q.shape
    s = jnp.einsum('blhd,bmhd->bhlm', q, k).astype(jnp.float32) * Dh ** -0.5
    p = jax.nn.softmax(s, axis=-1).astype(v.dtype)
    return jnp.einsum('bhlm,bmhd->blhd', p, v).reshape(B, L, H * Dh)


def _na_heads(t):
    B, T, _ = t.shape
    return t.reshape(B, T, NA_HEADS, NA_HEAD_DIM)


def _hg_heads(t):
    B, T, _ = t.shape
    return t.reshape(B, T, HG_HEADS, HG_DK).transpose(0, 2, 1, 3)


def _forget(z, lb):
    lb = lb.reshape(HG_HEADS, 1, HG_DK)
    f = lb + (1 - lb) * jax.nn.sigmoid(z.astype(jnp.float32))
    return jnp.log(f), 1 - f


def _time_flip(t, rev):
    return jnp.flip(t, axis=2) if rev else t


def gla_chunked(q, k, v, log_f, s0, with_output):
    B, H, T, _ = q.shape
    n_chunks = T // HG_CHUNK

    def chunks(t):
        return t.astype(jnp.float32).reshape(B, H, n_chunks, HG_CHUNK, t.shape[-1]).transpose(2, 0, 1, 3, 4)

    kc, vc, lfc = chunks(k), chunks(v), chunks(log_f)

    def update(S, k_n, v_n, b):
        b_last = b[:, :, -1:, :]
        k_dec = k_n * jnp.exp(b_last - b)
        return jnp.exp(b_last[:, :, 0, :])[..., None] * S + jnp.einsum('bhsd,bhsv->bhdv', k_dec, v_n)

    if not with_output:
        def state_step(S, xs):
            k_n, v_n, lf_n = xs
            return update(S, k_n, v_n, jnp.cumsum(lf_n, axis=2)), None
        s_fin, _ = lax.scan(state_step, s0, (kc, vc, lfc))
        return None, s_fin

    qc = chunks(q)
    causal = jnp.tril(jnp.ones((HG_CHUNK, HG_CHUNK), dtype=bool))

    def step(S, xs):
        q_n, k_n, v_n, lf_n = xs
        b = jnp.cumsum(lf_n, axis=2)
        pair = b[:, :, :, None, :] - b[:, :, None, :, :]
        decay = jnp.exp(jnp.where(causal[:, :, None], pair, -jnp.inf))
        A = jnp.einsum('bhtd,bhsd,bhtsd->bhts', q_n, k_n, decay)
        o = jnp.einsum('bhts,bhsv->bhtv', A, v_n) + jnp.einsum('bhtd,bhdv->bhtv', q_n * jnp.exp(b), S)
        return update(S, k_n, v_n, b), o

    s_fin, o = lax.scan(step, s0, (qc, kc, vc, lfc))
    o = o.transpose(1, 2, 0, 3, 4).reshape(B, H, T, -1)
    return o, s_fin


def _gated_out(o, g, norm_g):
    B, H, T, Dv = o.shape
    o = o * lax.rsqrt(jnp.mean(o * o, axis=-1, keepdims=True) + EPS) * norm_g.astype(jnp.float32).reshape(H, 1, Dv)
    o = o * jax.nn.silu(_hg_heads(g).astype(jnp.float32))
    return o.transpose(0, 2, 1, 3).reshape(B, T, H * Dv).astype(g.dtype)


def hgrn2_mixer(px, pc, lb, norm_g, need_ctx):
    qx, vx = jax.nn.silu(_hg_heads(px[0])), _hg_heads(px[3])
    qc, vc = jax.nn.silu(_hg_heads(pc[0])), _hg_heads(pc[3])
    s0 = jnp.zeros((qx.shape[0], HG_HEADS, HG_DK, HG_DK), jnp.float32)
    outs_x, outs_c = [], []
    for d in range(2):
        rev = d == 1
        lf_c, k_c = _forget(_hg_heads(pc[1 + d]), lb[d])
        lf_x, k_x = _forget(_hg_heads(px[1 + d]), lb[d])
        o_c, s_c = gla_chunked(_time_flip(qc, rev), _time_flip(k_c, rev), _time_flip(vc, rev),
                               _time_flip(lf_c, rev), s0, need_ctx)
        o_x, _ = gla_chunked(_time_flip(qx, rev), _time_flip(k_x, rev), _time_flip(vx, rev),
                             _time_flip(lf_x, rev), s_c, True)
        outs_x.append(_time_flip(o_x, rev))
        if need_ctx:
            outs_c.append(_time_flip(o_c, rev))
    y_x = _gated_out(outs_x[0] + outs_x[1], px[4], norm_g)
    y_c = _gated_out(outs_c[0] + outs_c[1], pc[4], norm_g) if need_ctx else None
    return y_x, y_c


def peer_ffn(h, wq, sub_keys, u, v):
    B, T, D = h.shape
    tok = h.reshape(-1, PEER_BLOCK, D)

    def block(t):
        q = (t @ wq).reshape(PEER_BLOCK, PEER_HEADS, 2, PEER_DQ // 2)
        s = jnp.einsum('tpcd,pckd->tpck', q, sub_keys).astype(jnp.float32)
        s1, i1 = lax.top_k(s[:, :, 0], PEER_TOPK)
        s2, i2 = lax.top_k(s[:, :, 1], PEER_TOPK)
        cand = (s1[..., :, None] + s2[..., None, :]).reshape(PEER_BLOCK, PEER_HEADS, PEER_TOPK * PEER_TOPK)
        cidx = (i1[..., :, None] * PEER_NKEYS + i2[..., None, :]).reshape(PEER_BLOCK, PEER_HEADS, PEER_TOPK * PEER_TOPK)
        top, pos = lax.top_k(cand, PEER_TOPK)
        idx = jnp.take_along_axis(cidx, pos, axis=-1)
        g = jax.nn.softmax(top, axis=-1)
        a = jax.nn.gelu(jnp.einsum('tpkd,td->tpk', u[idx], t).astype(jnp.float32), approximate=False)
        return jnp.einsum('tpk,tpkd->td', (g * a).astype(t.dtype), v[idx])

    return lax.map(block, tok).reshape(B, T, D)


def setup_inputs(seed: int = 0) -> dict:
    key = jax.random.key(seed)
    ks = jax.random.split(key, 18)

    def nrm(k, shape, scale):
        return jax.random.normal(k, shape, jnp.float32) * scale

    return {
        'x': nrm(ks[0], (BATCH, SEQ, D_MODEL), 1.0),
        'c': nrm(ks[1], (BATCH, D_MODEL), 1.0),
        'ctx': nrm(ks[2], (BATCH, CTX_LEN, D_MODEL), 1.0),
        'c_ctx': nrm(ks[3], (D_MODEL,), 1.0),
        'w_mod': nrm(ks[4], (DEPTH, D_MODEL, N_MOD * D_MODEL), 0.5 * D_MODEL ** -0.5),
        'b_mod': nrm(ks[5], (DEPTH, N_MOD * D_MODEL), 0.01),
        'norm1': 1.0 + nrm(ks[6], (DEPTH, D_MODEL), 0.05),
        'norm2': 1.0 + nrm(ks[7], (DEPTH, D_MODEL), 0.05),
        'w_in': nrm(ks[8], (DEPTH, D_MODEL, IN_COLS), D_MODEL ** -0.5),
        'w_out': nrm(ks[9], (DEPTH, MIX_WIDTH, D_MODEL), MIX_WIDTH ** -0.5),
        'na_rpb': nrm(ks[10], (DEPTH, NA_HEADS, 2 * NA_KH - 1, 2 * NA_KW - 1), 0.2),
        'hg_lb': nrm(ks[11], (DEPTH + 1, 2, HG_WIDTH), 1.0),
        'hg_norm': 1.0 + nrm(ks[12], (DEPTH, HG_WIDTH), 0.05),
        'peer_wq': nrm(ks[13], (DEPTH, D_MODEL, PEER_HEADS * PEER_DQ), D_MODEL ** -0.5),
        'peer_keys': nrm(ks[14], (DEPTH, PEER_HEADS, 2, PEER_NKEYS, PEER_DQ // 2), (PEER_DQ // 2) ** -0.5),
        'peer_u': nrm(ks[15], (DEPTH, PEER_EXPERTS, D_MODEL), D_MODEL ** -0.5),
        'peer_v': nrm(ks[16], (DEPTH, PEER_EXPERTS, D_MODEL), PEER_HEADS ** -0.5),
        'norm_f': 1.0 + nrm(ks[17], (D_MODEL,), 0.05),
    }


def reference(x, c, ctx, c_ctx, w_mod, b_mod, norm1, norm2, w_in, w_out, na_rpb, hg_lb, hg_norm,
              peer_wq, peer_keys, peer_u, peer_v, norm_f):
    lb_all = jnp.cumsum(jax.nn.softmax(hg_lb.astype(jnp.float32), axis=0), axis=0)
    splits = [NA_WIDTH, 2 * NA_WIDTH, 3 * NA_WIDTH] + [3 * NA_WIDTH + j * HG_WIDTH for j in range(1, 5)]
    for l in range(DEPTH):
        need_ctx = l < DEPTH - 1
        mod_x = (jax.nn.silu(c) @ w_mod[l] + b_mod[l])[:, None, :]
        mod_c = jax.nn.silu(c_ctx) @ w_mod[l] + b_mod[l]
        sh1, sc1, gt1, sh2, sc2, gt2 = jnp.split(mod_x, N_MOD, axis=-1)
        csh1, csc1, cgt1, csh2, csc2, cgt2 = jnp.split(mod_c, N_MOD, axis=-1)

        px = jnp.split(modulate(rms_norm(x, norm1[l]), sh1, sc1) @ w_in[l], splits, axis=-1)
        pc = jnp.split(modulate(rms_norm(ctx, norm1[l]), csh1, csc1) @ w_in[l], splits, axis=-1)
        k_ctx, v_ctx = _na_heads(pc[1]), _na_heads(pc[2])
        na_x = neighbourhood_attention(_na_heads(px[0]), _na_heads(px[1]), _na_heads(px[2]),
                                       k_ctx, v_ctx, na_rpb[l])
        hg_x, hg_c = hgrn2_mixer(px[3:], pc[3:], lb_all[l], hg_norm[l], need_ctx)
        x = x + gt1 * (jnp.concatenate([na_x, hg_x], axis=-1) @ w_out[l])
        if need_ctx:
            na_c = context_attention(_na_heads(pc[0]), k_ctx, v_ctx)
            ctx = ctx + cgt1 * (jnp.concatenate([na_c, hg_c], axis=-1) @ w_out[l])
            ctx = ctx + cgt2 * peer_ffn(modulate(rms_norm(ctx, norm2[l]), csh2, csc2),
                                        peer_wq[l], peer_keys[l], peer_u[l], peer_v[l])

        x = x + gt2 * peer_ffn(modulate(rms_norm(x, norm2[l]), sh2, sc2),
                               peer_wq[l], peer_keys[l], peer_u[l], peer_v[l])
    return rms_norm(x, norm_f)
```

```python
import functools

import numpy as np
import jax
import jax.numpy as jnp
from jax import lax
from jax.experimental import pallas as pl
from jax.experimental.pallas import tpu as pltpu

F32 = jnp.float32
BF16 = jnp.bfloat16

D_MODEL = 1024
SEQ = 2048
CTX_LEN = 256
GRID_W = 64
ROWS = SEQ // GRID_W

NA_HEADS = 8
NA_HEAD_DIM = 64
NA_WIDTH = NA_HEADS * NA_HEAD_DIM
NA_KH = 8
NA_KW = 16
NA_WIN = NA_KH * GRID_W

HG_HEADS = 4
HG_DK = 128
HG_WIDTH = HG_HEADS * HG_DK
HG_CHUNK = 64
HG_SUB = 16

IN_COLS = 3 * NA_WIDTH + 5 * HG_WIDTH

PEER_HEADS = 8
PEER_NKEYS = 128
PEER_EXPERTS = PEER_NKEYS * PEER_NKEYS
PEER_TOPK = 16
PEER_HALF = 128
PEER_QW = PEER_HEADS * PEER_HALF

N_MOD = 6
EPS = 1e-6
NEG_INF = float("-inf")

VMEM_LIMIT = 56 * 1024 * 1024


def _cparams(*sem):
    return pltpu.CompilerParams(dimension_semantics=sem, vmem_limit_bytes=VMEM_LIMIT)


def _silu(x):
    return x * jax.nn.sigmoid(x)


def _dot(a, b):
    return jnp.dot(a, b, preferred_element_type=F32)


def _dot_nt(a, b):
    return lax.dot_general(a, b, (((1,), (1,)), ((), ())), preferred_element_type=F32)


def _dot_tn(a, b):
    return lax.dot_general(a, b, (((0,), (0,)), ((), ())), preferred_element_type=F32)


def _mod_body(c_ref, w_ref, b_ref, o_ref):
    o_ref[...] = _dot(_silu(c_ref[...]).astype(BF16), w_ref[...].astype(BF16)) + b_ref[...]


def _mod(c_rows, w_mod, b_mod):
    n = c_rows.shape[0]
    return pl.pallas_call(
        _mod_body,
        grid=(N_MOD,),
        in_specs=[pl.BlockSpec((n, D_MODEL), lambda j: (0, 0)),
                  pl.BlockSpec((D_MODEL, D_MODEL), lambda j: (0, j)),
                  pl.BlockSpec((1, D_MODEL), lambda j: (0, j))],
        out_specs=pl.BlockSpec((n, D_MODEL), lambda j: (0, j)),
        out_shape=jax.ShapeDtypeStruct((n, N_MOD * D_MODEL), F32),
        compiler_params=_cparams("arbitrary"),
        name="mod",
    )(c_rows, w_mod, b_mod)


def _norm_mod(x, g, shift, scale):
    y = x * lax.rsqrt(jnp.mean(x * x, axis=-1, keepdims=True) + EPS) * g
    return y * (1.0 + scale) + shift


def _inproj_body(x_ref, sh_ref, sc_ref, g_ref, w_ref, qkv_ref, hg_ref):
    h = _norm_mod(x_ref[0], g_ref[...], sh_ref[0], sc_ref[0]).astype(BF16)
    p = _dot(h, w_ref[:, 0:1024])
    qkv_ref[0, :, 0:512] = (p[:, 0:512] * (NA_HEAD_DIM ** -0.5)).astype(BF16)
    qkv_ref[0, :, 512:1024] = p[:, 512:1024].astype(BF16)
    p = _dot(h, w_ref[:, 1024:2048])
    qkv_ref[0, :, 1024:1536] = p[:, 0:512].astype(BF16)
    hg_ref[0, :, 0:512] = p[:, 512:1024]
    hg_ref[0, :, 512:1536] = _dot(h, w_ref[:, 2048:3072])
    hg_ref[0, :, 1536:2560] = _dot(h, w_ref[:, 3072:4096])


def _inproj(x, mod3, mod_row, g, w_in, tm):
    b_, t_, _ = x.shape
    return pl.pallas_call(
        _inproj_body,
        grid=(b_, t_ // tm),
        in_specs=[pl.BlockSpec((1, tm, D_MODEL), lambda b, i: (b, i, 0)),
                  pl.BlockSpec((1, 1, D_MODEL), lambda b, i: (mod_row(b), 0, 0)),
                  pl.BlockSpec((1, 1, D_MODEL), lambda b, i: (mod_row(b), 0, 1)),
                  pl.BlockSpec((1, D_MODEL), lambda b, i: (0, 0)),
                  pl.BlockSpec((D_MODEL, IN_COLS), lambda b, i: (0, 0))],
        out_specs=[pl.BlockSpec((1, tm, 3 * NA_WIDTH), lambda b, i: (b, i, 0)),
                   pl.BlockSpec((1, tm, 5 * HG_WIDTH), lambda b, i: (b, i, 0))],
        out_shape=[jax.ShapeDtypeStruct((b_, t_, 3 * NA_WIDTH), BF16),
                   jax.ShapeDtypeStruct((b_, t_, 5 * HG_WIDTH), F32)],
        compiler_params=_cparams("parallel", "parallel"),
        name="inproj",
    )(x, mod3, mod3, g, w_in)


def _na_row_start(r):
    return jnp.clip(r - NA_KH // 2, 0, ROWS - NA_KH)


def _na_body(q_ref, k_ref, v_ref, kc_ref, vc_ref, bias_ref, o_ref):
    r = pl.program_id(1)
    start = pl.multiple_of(_na_row_start(r) * GRID_W, GRID_W)
    lane = lax.broadcasted_iota(jnp.int32, (GRID_W, 2 * NA_HEAD_DIM), 1)
    low = lane < NA_HEAD_DIM
    for hp in range(NA_HEADS // 2):
        cols = slice(hp * 128, (hp + 1) * 128)
        q2 = q_ref[0, :, cols]
        kw = k_ref[0, pl.ds(start, NA_WIN), cols]
        vw = v_ref[0, pl.ds(start, NA_WIN), cols]
        kc = kc_ref[0, :, cols]
        vc = vc_ref[0, :, cols]
        outs = []
        for sub in range(2):
            qh = jnp.where(low if sub == 0 else ~low, q2, jnp.zeros_like(q2))
            s_w = _dot_nt(qh, kw) + bias_ref[0, 2 * hp + sub]
            s_c = _dot_nt(qh, kc)
            m = jnp.maximum(jnp.max(s_w, axis=-1, keepdims=True), jnp.max(s_c, axis=-1, keepdims=True))
            p_w = jnp.exp(s_w - m)
            p_c = jnp.exp(s_c - m)
            l = jnp.sum(p_w, axis=-1, keepdims=True) + jnp.sum(p_c, axis=-1, keepdims=True)
            o = _dot(p_w.astype(BF16), vw) + _dot(p_c.astype(BF16), vc)
            outs.append(o / l)
        o_ref[0, :, cols] = jnp.where(low, outs[0], outs[1]).astype(BF16)


def _na_bias_table(rpb):
    qc = np.arange(GRID_W)[:, None]
    kc = np.arange(GRID_W)[None, :]
    cstart = np.clip(qc - NA_KW // 2, 0, GRID_W - NA_KW)
    valid = (kc >= cstart) & (kc < cstart + NA_KW)
    dc = np.clip(kc - qc + NA_KW - 1, 0, 2 * NA_KW - 2)
    dr = np.arange(NA_KH)[:, None] + np.arange(NA_KH)[None, :]
    t = rpb.astype(F32)[:, dr][:, :, :, dc]
    t = jnp.where(valid[None, None, None], t, NEG_INF)
    return t.transpose(1, 0, 3, 2, 4).reshape(NA_KH, NA_HEADS, GRID_W, NA_WIN)


def _na(qkv_x, qkv_c, bias_tab):
    b_ = qkv_x.shape[0]

    def d0(r):
        return _na_row_start(r) - r + NA_KH - 1

    return pl.pallas_call(
        _na_body,
        grid=(b_, ROWS),
        in_specs=[pl.BlockSpec((1, GRID_W, NA_WIDTH), lambda b, r: (b, r, 0)),
                  pl.BlockSpec((1, SEQ, NA_WIDTH), lambda b, r: (b, 0, 1)),
                  pl.BlockSpec((1, SEQ, NA_WIDTH), lambda b, r: (b, 0, 2)),
                  pl.BlockSpec((1, CTX_LEN, NA_WIDTH), lambda b, r: (b, 0, 1)),
                  pl.BlockSpec((1, CTX_LEN, NA_WIDTH), lambda b, r: (b, 0, 2)),
                  pl.BlockSpec((1, NA_HEADS, GRID_W, NA_WIN), lambda b, r: (d0(r), 0, 0, 0))],
        out_specs=pl.BlockSpec((1, GRID_W, NA_WIDTH), lambda b, r: (b, r, 0)),
        out_shape=jax.ShapeDtypeStruct((b_, SEQ, NA_WIDTH), BF16),
        compiler_params=_cparams("parallel", "arbitrary"),
        name="na",
    )(qkv_x, qkv_x, qkv_x, qkv_c, qkv_c, bias_tab)


def _hg_gates(z, lbv):
    f = lbv + (1.0 - lbv) * jax.nn.sigmoid(z)
    return jnp.log(f), 1.0 - f


def _hg_cumsum(lf, tri):
    hi = lf.astype(BF16)
    r1 = lf - hi.astype(F32)
    mid = r1.astype(BF16)
    lo = (r1 - mid.astype(F32)).astype(BF16)
    return _dot(tri, hi) + _dot(tri, mid) + _dot(tri, lo)


def _hg_tri(rev):
    t = lax.broadcasted_iota(jnp.int32, (HG_CHUNK, HG_CHUNK), 0)
    s = lax.broadcasted_iota(jnp.int32, (HG_CHUNK, HG_CHUNK), 1)
    return jnp.where((s >= t) if rev else (s <= t), 1.0, 0.0).astype(BF16)


def _hg_state_update(st, k, v, b, rev):
    b_end = b[0:1] if rev else b[HG_CHUNK - 1:HG_CHUNK]
    kdec = k * jnp.exp(b_end - b)
    return st * jnp.exp(b_end) + _dot_tn(v.astype(BF16), kdec.astype(BF16))


def _hg_group_row(a, s):
    a4 = a.reshape(HG_CHUNK // HG_SUB, HG_SUB, HG_DK)
    return jnp.broadcast_to(a4[:, s:s + 1, :], a4.shape).reshape(HG_CHUNK, HG_DK)


def _hg_intra(qa, k, v, b, rev):
    t = lax.broadcasted_iota(jnp.int32, (HG_CHUNK, HG_DK), 0)
    tl = t % HG_SUB
    nsub = HG_CHUNK // HG_SUB
    a_off = jnp.zeros((HG_CHUNK, HG_CHUNK), F32)
    for j in (range(1, nsub) if rev else range(nsub - 1)):
        edge = j * HG_SUB if rev else j * HG_SUB + HG_SUB - 1
        e = b[edge:edge + 1]
        qmask = (t < j * HG_SUB) if rev else (t >= (j + 1) * HG_SUB)
        kmask = (t >= j * HG_SUB) & (t < (j + 1) * HG_SUB)
        qd = qa * jnp.exp(jnp.where(qmask, b - e, NEG_INF))
        kd = k * jnp.exp(jnp.where(kmask, e - b, NEG_INF))
        a_off = a_off + _dot_nt(qd.astype(BF16), kd.astype(BF16))
    ps = []
    for s in range(HG_SUB):
        mask = (tl <= s) if rev else (tl >= s)
        dec = jnp.exp(jnp.where(mask, b - _hg_group_row(b, s), NEG_INF))
        ps.append((qa * _hg_group_row(k, s) * dec).astype(BF16))
    p = jnp.concatenate(ps, axis=0)
    rs = _dot(p, jnp.ones((HG_DK, HG_DK), BF16))
    lane = lax.broadcasted_iota(jnp.int32, (HG_CHUNK, HG_DK), 1)
    a_diag = jnp.zeros((HG_CHUNK, HG_DK), F32)
    for s in range(HG_SUB):
        a_diag = a_diag + jnp.where(lane == (t - tl) + s, rs[s * HG_CHUNK:(s + 1) * HG_CHUNK], 0.0)
    a = a_off + a_diag[:, 0:HG_CHUNK]
    return _dot(a.astype(BF16), v.astype(BF16))


def _hgrn_body(q_ref, ff_ref, fb_ref, i_ref, g_ref, cff_ref, cfb_ref, ci_ref, lb_ref, ng_ref, o_ref, acc_ref):
    lbr = lb_ref[...]
    mx = jnp.maximum(lbr[0], lbr[1])
    e0 = jnp.exp(lbr[0] - mx)
    lb_dirs = e0 / (e0 + jnp.exp(lbr[1] - mx))
    n_ctx = CTX_LEN // HG_CHUNK
    n_x = SEQ // HG_CHUNK
    for d in range(2):
        rev = d == 1
        lbv = lb_dirs[d:d + 1]
        tri = _hg_tri(rev)
        zc_ref, zx_ref = (cfb_ref, fb_ref) if rev else (cff_ref, ff_ref)

        def ctx_step(n, st, rev=rev, lbv=lbv, tri=tri, zc_ref=zc_ref):
            c = (n_ctx - 1 - n) if rev else n
            rows = pl.ds(pl.multiple_of(c * HG_CHUNK, HG_CHUNK), HG_CHUNK)
            lf, k = _hg_gates(zc_ref[0, rows, :], lbv)
            return _hg_state_update(st, k, ci_ref[0, rows, :], _hg_cumsum(lf, tri), rev)

        def x_step(n, st, rev=rev, lbv=lbv, tri=tri, zx_ref=zx_ref):
            c = (n_x - 1 - n) if rev else n
            rows = pl.ds(pl.multiple_of(c * HG_CHUNK, HG_CHUNK), HG_CHUNK)
            lf, k = _hg_gates(zx_ref[0, rows, :], lbv)
            v = i_ref[0, rows, :]
            qa = _silu(q_ref[0, rows, :])
            b = _hg_cumsum(lf, tri)
            o = _dot_nt((qa * jnp.exp(b)).astype(BF16), st.astype(BF16)) + _hg_intra(qa, k, v, b, rev)
            if rev:
                acc_ref[rows, :] += o
            else:
                acc_ref[rows, :] = o
            return _hg_state_update(st, k, v, b, rev)

        st = lax.fori_loop(0, n_ctx, ctx_step, jnp.zeros((HG_DK, HG_DK), F32))
        lax.fori_loop(0, n_x, x_step, st)

    ng = ng_ref[...]

    def fin(n, carry):
        rows = pl.ds(pl.multiple_of(n * 256, 256), 256)
        o = acc_ref[rows, :]
        y = o * lax.rsqrt(jnp.mean(o * o, axis=-1, keepdims=True) + EPS) * ng
        o_ref[0, rows, :] = (y * _silu(g_ref[0, rows, :])).astype(BF16)
        return carry

    lax.fori_loop(0, SEQ // 256, fin, 0)


def _hgrn(hg_x, hg_c, hg_lb, hg_norm):
    b_ = hg_x.shape[0]

    def xs(j):
        return pl.BlockSpec((1, SEQ, HG_DK), lambda b, h: (b, 0, j * HG_HEADS + h))

    def cs(j):
        return pl.BlockSpec((1, CTX_LEN, HG_DK), lambda b, h: (b, 0, j * HG_HEADS + h))

    return pl.pallas_call(
        _hgrn_body,
        grid=(b_, HG_HEADS),
        in_specs=[xs(0), xs(1), xs(2), xs(3), xs(4), cs(1), cs(2), cs(3),
                  pl.BlockSpec((2, 2, HG_DK), lambda b, h: (0, 0, h)),
                  pl.BlockSpec((1, HG_DK), lambda b, h: (0, h))],
        out_specs=pl.BlockSpec((1, SEQ, HG_DK), lambda b, h: (b, 0, h)),
        out_shape=jax.ShapeDtypeStruct((b_, SEQ, HG_WIDTH), BF16),
        scratch_shapes=[pltpu.VMEM((SEQ, HG_DK), F32)],
        compiler_params=_cparams("parallel", "parallel"),
        name="hgrn",
    )(hg_x, hg_x, hg_x, hg_x, hg_x, hg_c, hg_c, hg_c, hg_lb, hg_norm)


def _mix_body(na_ref, hg_ref, x_ref, gt_ref, g2_ref, sh_ref, sc_ref, wo_ref, wq_ref, k1_ref, k2_ref, k2h_ref,
              x1_ref, h2_ref, s1_ref, s2_ref, s2h_ref):
    mix = _dot(na_ref[0], wo_ref[0:NA_WIDTH, :]) + _dot(hg_ref[0], wo_ref[NA_WIDTH:, :])
    x1 = x_ref[0] + gt_ref[0] * mix
    x1_ref[0] = x1
    h2 = _norm_mod(x1, g2_ref[...], sh_ref[0], sc_ref[0]).astype(BF16)
    h2_ref[0] = h2
    q = _dot(h2, wq_ref[...]).astype(BF16)
    q1 = q[:, 0:PEER_QW]
    q2 = q[:, PEER_QW:]
    s1_ref[0] = _dot_nt(k1_ref[...], q1)
    s2_ref[0] = _dot_nt(k2_ref[...], q2)
    s2h_ref[0] = _dot_nt(k2h_ref[...], q2)


def _mix(na, hg, x, mod3, g2, w_out, wq, k1, k2, k2h, tm):
    b_ = x.shape[0]
    tok = lambda b, i: (b, i, 0)
    tokt = lambda b, i: (b, 0, i)
    full = lambda b, i: (0, 0)
    return pl.pallas_call(
        _mix_body,
        grid=(b_, SEQ // tm),
        in_specs=[pl.BlockSpec((1, tm, NA_WIDTH), tok),
                  pl.BlockSpec((1, tm, HG_WIDTH), tok),
                  pl.BlockSpec((1, tm, D_MODEL), tok),
                  pl.BlockSpec((1, 1, D_MODEL), lambda b, i: (b, 0, 2)),
                  pl.BlockSpec((1, D_MODEL), full),
                  pl.BlockSpec((1, 1, D_MODEL), lambda b, i: (b, 0, 3)),
                  pl.BlockSpec((1, 1, D_MODEL), lambda b, i: (b, 0, 4)),
                  pl.BlockSpec((D_MODEL, D_MODEL), full),
                  pl.BlockSpec((D_MODEL, 2 * PEER_QW), full),
                  pl.BlockSpec((PEER_QW, PEER_QW), full),
                  pl.BlockSpec((PEER_QW, PEER_QW), full),
                  pl.BlockSpec((PEER_QW, PEER_QW), full)],
        out_specs=[pl.BlockSpec((1, tm, D_MODEL), tok),
                   pl.BlockSpec((1, tm, D_MODEL), tok),
                   pl.BlockSpec((1, PEER_QW, tm), tokt),
                   pl.BlockSpec((1, PEER_QW, tm), tokt),
                   pl.BlockSpec((1, PEER_QW, tm), tokt)],
        out_shape=[jax.ShapeDtypeStruct((b_, SEQ, D_MODEL), F32),
                   jax.ShapeDtypeStruct((b_, SEQ, D_MODEL), BF16),
                   jax.ShapeDtypeStruct((b_, PEER_QW, SEQ), F32),
                   jax.ShapeDtypeStruct((b_, PEER_QW, SEQ), F32),
                   jax.ShapeDtypeStruct((b_, PEER_QW, SEQ), F32)],
        compiler_params=_cparams("parallel", "parallel"),
        name="mix",
    )(na, hg, x, mod3, g2, mod3, mod3, w_out, wq, k1, k2, k2h)


def _top_values(s, n):
    vals = [jnp.max(s, axis=0)]
    for _ in range(n - 1):
        vals.append(jnp.max(jnp.where(s < vals[-1][None], s, NEG_INF), axis=0))
    return vals


def _route_body(s1_ref, s2_ref, s2h_ref, c_ref, e1_ref, e2_ref):
    tl = s1_ref.shape[-1]
    s1 = s1_ref[0].reshape(PEER_NKEYS, PEER_HEADS, tl)
    s2 = s2_ref[0].reshape(PEER_NKEYS, PEER_HEADS, tl)
    n = PEER_TOPK + 1
    a = _top_values(s1, n)
    b = _top_values(s2, n)
    cands = jnp.stack([a[k] + b[l] for k in range(n) for l in range(n) if (k + 1) * (l + 1) <= n], axis=0)
    tops = _top_values(cands, n)
    thr = tops[PEER_TOPK - 1]
    cut = 0.5 * (thr + tops[PEER_TOPK])
    top = a[0] + b[0]
    z = jnp.sum(jnp.where(cands >= thr[None], jnp.exp(cands - top[None]), 0.0), axis=0)
    c_ref[0] = (cut[None] - s1).reshape(PEER_QW, tl)
    e1_ref[0] = (jnp.exp(s1 - a[0][None]) / z[None]).reshape(PEER_QW, tl)
    b0 = b[0]
    for p in range(PEER_HEADS):
        rows = slice(p * PEER_NKEYS, (p + 1) * PEER_NKEYS)
        e2_ref[0, rows, :] = jnp.exp(s2h_ref[0, rows, :] - b0[p:p + 1])


def _route(s1, s2, s2h, tl):
    b_ = s1.shape[0]
    spec = pl.BlockSpec((1, PEER_QW, tl), lambda b, i: (b, 0, i))
    shp = jax.ShapeDtypeStruct((b_, PEER_QW, SEQ), F32)
    return pl.pallas_call(
        _route_body,
        grid=(b_, SEQ // tl),
        in_specs=[spec, spec, spec],
        out_specs=[spec, spec, spec],
        out_shape=[shp, shp, shp],
        compiler_params=_cparams("parallel", "parallel"),
        name="route",
    )(s1, s2, s2h)


PEER_EC = 1024


def _gelu(a):
    return 0.5 * a * (1.0 + lax.erf(a * 0.7071067811865476))


def _peer_body(h2_ref, u_ref, vt_ref, c_ref, e1_ref, s2_ref, e2_ref, x1_ref, gt_ref, gf_ref, y_ref, acc_ref, w_ref):
    e = pl.program_id(2)

    @pl.when(e == 0)
    def _():
        acc_ref[...] = jnp.zeros_like(acc_ref)

    at = _dot_nt(u_ref[...], h2_ref[0])
    for il in range(PEER_EC // PEER_NKEYS):
        g = None
        for p in range(PEER_HEADS):
            row = il * PEER_HEADS + p
            keys = slice(p * PEER_NKEYS, (p + 1) * PEER_NKEYS)
            sel = s2_ref[0, keys, :] >= c_ref[0, row:row + 1, :]
            gp = jnp.where(sel, e2_ref[0, keys, :] * e1_ref[0, row:row + 1, :], 0.0)
            g = gp if g is None else g + gp
        rows = slice(il * PEER_NKEYS, (il + 1) * PEER_NKEYS)
        w_ref[rows, :] = (g * _gelu(at[rows, :])).astype(BF16)
    acc_ref[...] += _dot(vt_ref[...], w_ref[...])

    @pl.when(e == pl.num_programs(2) - 1)
    def _():
        xo = x1_ref[0] + gt_ref[0] * acc_ref[...].T
        y_ref[0] = xo * lax.rsqrt(jnp.mean(xo * xo, axis=-1, keepdims=True) + EPS) * gf_ref[...]


def _peer(h2, u, vt, c, e1, s2h, e2, x1, mod3, gf, tm):
    b_ = h2.shape[0]
    ne = PEER_EXPERTS // PEER_EC
    rows_per = PEER_EC // PEER_NKEYS * PEER_HEADS
    tok = lambda b, i, e: (b, i, 0)
    tokt = lambda b, i, e: (b, 0, i)
    return pl.pallas_call(
        _peer_body,
        grid=(b_, SEQ // tm, ne),
        in_specs=[pl.BlockSpec((1, tm, D_MODEL), tok),
                  pl.BlockSpec((PEER_EC, D_MODEL), lambda b, i, e: (e, 0)),
                  pl.BlockSpec((D_MODEL, PEER_EC), lambda b, i, e: (0, e)),
                  pl.BlockSpec((1, rows_per, tm), lambda b, i, e: (b, e, i)),
                  pl.BlockSpec((1, rows_per, tm), lambda b, i, e: (b, e, i)),
                  pl.BlockSpec((1, PEER_QW, tm), tokt),
                  pl.BlockSpec((1, PEER_QW, tm), tokt),
                  pl.BlockSpec((1, tm, D_MODEL), tok),
                  pl.BlockSpec((1, 1, D_MODEL), lambda b, i, e: (b, 0, 5)),
                  pl.BlockSpec((1, D_MODEL), lambda b, i, e: (0, 0))],
        out_specs=pl.BlockSpec((1, tm, D_MODEL), tok),
        out_shape=jax.ShapeDtypeStruct((b_, SEQ, D_MODEL), F32),
        scratch_shapes=[pltpu.VMEM((D_MODEL, tm), F32), pltpu.VMEM((PEER_EC, tm), BF16)],
        compiler_params=_cparams("parallel", "parallel", "arbitrary"),
        name="peer",
    )(h2, u, vt, c, e1, s2h, e2, x1, mod3, gf)


def _peer_key_layouts(keys):
    ar = np.arange(PEER_HEADS)
    out = []
    for c, key_major in ((0, True), (1, True), (1, False)):
        kc = keys[:, c].astype(BF16)
        if key_major:
            m = jnp.zeros((PEER_NKEYS, PEER_HEADS, PEER_HEADS, PEER_HALF), BF16)
            m = m.at[:, ar, ar, :].set(kc.transpose(1, 0, 2))
        else:
            m = jnp.zeros((PEER_HEADS, PEER_NKEYS, PEER_HEADS, PEER_HALF), BF16)
            m = m.at[ar, :, ar, :].set(kc)
        out.append(m.reshape(PEER_QW, PEER_QW))
    return out


def kernel(x, c, ctx, c_ctx, w_mod, b_mod, norm1, norm2, w_in, w_out, na_rpb, hg_lb, hg_norm,
           peer_wq, peer_keys, peer_u, peer_v, norm_f):
    assert w_mod.shape[0] == 1 and hg_lb.shape[0] == 2, "single-layer configuration"
    b_ = x.shape[0]
    pad = (-(b_ + 1)) % 8
    c_rows = jnp.concatenate([c, c_ctx[None], jnp.zeros((pad, D_MODEL), F32)], axis=0)
    mod = _mod(c_rows, w_mod[0], b_mod[0][None])
    mod3 = mod.reshape(mod.shape[0], 1, N_MOD * D_MODEL)

    w_in_b = w_in[0].astype(BF16)
    g1 = norm1[0][None]
    qkv_x, hg_x = _inproj(x, mod3, lambda b: b, g1, w_in_b, 512)
    qkv_c, hg_c = _inproj(ctx, mod3, lambda b: b_, g1, w_in_b, CTX_LEN)

    na = _na(qkv_x, qkv_c, _na_bias_table(na_rpb[0]))
    hg = _hgrn(hg_x, hg_c, hg_lb, hg_norm[0][None])

    wq = peer_wq[0].reshape(D_MODEL, PEER_HEADS, 2, PEER_HALF).transpose(0, 2, 1, 3)
    wq = wq.reshape(D_MODEL, 2 * PEER_QW).astype(BF16)
    k1, k2, k2h = _peer_key_layouts(peer_keys[0])
    x1, h2, s1, s2, s2h = _mix(na, hg, x, mod3, norm2[0][None], w_out[0].astype(BF16), wq, k1, k2, k2h, 256)

    cthr, e1, e2 = _route(s1, s2, s2h, 256)
    u = peer_u[0].astype(BF16)
    vt = peer_v[0].T.astype(BF16)
    return _peer(h2, u, vt, cthr, e1, s2h, e2, x1, mod3, norm_f[None], 512)
```

```python
import functools

import numpy as np
import jax
import jax.numpy as jnp
from jax import lax
from jax.experimental import pallas as pl
from jax.experimental.pallas import tpu as pltpu

F32 = jnp.float32
BF16 = jnp.bfloat16

D_MODEL = 1024
SEQ = 2048
CTX_LEN = 256
GRID_W = 64
ROWS = SEQ // GRID_W

NA_HEADS = 8
NA_HEAD_DIM = 64
NA_WIDTH = NA_HEADS * NA_HEAD_DIM
NA_KH = 8
NA_KW = 16
NA_WIN = NA_KH * GRID_W

HG_HEADS = 4
HG_DK = 128
HG_WIDTH = HG_HEADS * HG_DK
HG_CHUNK = 64
HG_SUB = 16

IN_COLS = 3 * NA_WIDTH + 5 * HG_WIDTH

PEER_HEADS = 8
PEER_NKEYS = 128
PEER_EXPERTS = PEER_NKEYS * PEER_NKEYS
PEER_TOPK = 16
PEER_HALF = 128
PEER_QW = PEER_HEADS * PEER_HALF

N_MOD = 6
EPS = 1e-6
NEG_INF = float("-inf")

VMEM_LIMIT = 56 * 1024 * 1024


def _cparams(*sem, flags=None):
    return pltpu.CompilerParams(dimension_semantics=sem, vmem_limit_bytes=VMEM_LIMIT, flags=flags)


def _silu(x):
    return x * jax.nn.sigmoid(x)


def _dot(a, b):
    return jnp.dot(a, b, preferred_element_type=F32)


def _dot_nt(a, b):
    return lax.dot_general(a, b, (((1,), (1,)), ((), ())), preferred_element_type=F32)


def _dot_tn(a, b):
    return lax.dot_general(a, b, (((0,), (0,)), ((), ())), preferred_element_type=F32)


def _mod_body(c_ref, w_ref, b_ref, o_ref):
    o_ref[...] = _dot(_silu(c_ref[...]).astype(BF16), w_ref[...].astype(BF16)) + b_ref[...]


def _mod(c_rows, w_mod, b_mod):
    n = c_rows.shape[0]
    return pl.pallas_call(
        _mod_body,
        grid=(N_MOD,),
        in_specs=[pl.BlockSpec((n, D_MODEL), lambda j: (0, 0)),
                  pl.BlockSpec((D_MODEL, D_MODEL), lambda j: (0, j)),
                  pl.BlockSpec((1, D_MODEL), lambda j: (0, j))],
        out_specs=pl.BlockSpec((n, D_MODEL), lambda j: (0, j)),
        out_shape=jax.ShapeDtypeStruct((n, N_MOD * D_MODEL), F32),
        compiler_params=_cparams("arbitrary"),
        name="mod",
    )(c_rows, w_mod, b_mod)


def _norm_mod(x, g, shift, scale):
    y = x * lax.rsqrt(jnp.mean(x * x, axis=-1, keepdims=True) + EPS) * g
    return y * (1.0 + scale) + shift


def _inproj_body(x_ref, sh_ref, sc_ref, g_ref, w_ref, qkv_ref, hg_ref):
    h = _norm_mod(x_ref[0], g_ref[...], sh_ref[0], sc_ref[0]).astype(BF16)
    p = _dot(h, w_ref[:, 0:1024])
    qkv_ref[0, :, 0:512] = (p[:, 0:512] * (NA_HEAD_DIM ** -0.5)).astype(BF16)
    qkv_ref[0, :, 512:1024] = p[:, 512:1024].astype(BF16)
    p = _dot(h, w_ref[:, 1024:2048])
    qkv_ref[0, :, 1024:1536] = p[:, 0:512].astype(BF16)
    hg_ref[0, :, 0:512] = p[:, 512:1024]
    hg_ref[0, :, 512:1536] = _dot(h, w_ref[:, 2048:3072])
    hg_ref[0, :, 1536:2560] = _dot(h, w_ref[:, 3072:4096])


def _inproj(x, mod3, mod_row, g, w_in, tm):
    b_, t_, _ = x.shape
    return pl.pallas_call(
        _inproj_body,
        grid=(b_, t_ // tm),
        in_specs=[pl.BlockSpec((1, tm, D_MODEL), lambda b, i: (b, i, 0)),
                  pl.BlockSpec((1, 1, D_MODEL), lambda b, i: (mod_row(b), 0, 0)),
                  pl.BlockSpec((1, 1, D_MODEL), lambda b, i: (mod_row(b), 0, 1)),
                  pl.BlockSpec((1, D_MODEL), lambda b, i: (0, 0)),
                  pl.BlockSpec((D_MODEL, IN_COLS), lambda b, i: (0, 0))],
        out_specs=[pl.BlockSpec((1, tm, 3 * NA_WIDTH), lambda b, i: (b, i, 0)),
                   pl.BlockSpec((1, tm, 5 * HG_WIDTH), lambda b, i: (b, i, 0))],
        out_shape=[jax.ShapeDtypeStruct((b_, t_, 3 * NA_WIDTH), BF16),
                   jax.ShapeDtypeStruct((b_, t_, 5 * HG_WIDTH), F32)],
        compiler_params=_cparams("parallel", "parallel"),
        name="inproj",
    )(x, mod3, mod3, g, w_in)


def _na_row_start(r):
    return jnp.clip(r - NA_KH // 2, 0, ROWS - NA_KH)


def _na_body(q_ref, k_ref, v_ref, kc_ref, vc_ref, bias_ref, o_ref):
    r = pl.program_id(1)
    start = pl.multiple_of(_na_row_start(r) * GRID_W, GRID_W)
    lane = lax.broadcasted_iota(jnp.int32, (GRID_W, 2 * NA_HEAD_DIM), 1)
    low = lane < NA_HEAD_DIM
    for hp in range(NA_HEADS // 2):
        cols = slice(hp * 128, (hp + 1) * 128)
        q2 = q_ref[0, :, cols]
        kw = k_ref[0, pl.ds(start, NA_WIN), cols]
        vw = v_ref[0, pl.ds(start, NA_WIN), cols]
        kc = kc_ref[0, :, cols]
        vc = vc_ref[0, :, cols]
        outs = []
        for sub in range(2):
            qh = jnp.where(low if sub == 0 else ~low, q2, jnp.zeros_like(q2))
            s_w = _dot_nt(qh, kw) + bias_ref[0, 2 * hp + sub]
            s_c = _dot_nt(qh, kc)
            m = jnp.maximum(jnp.max(s_w, axis=-1, keepdims=True), jnp.max(s_c, axis=-1, keepdims=True))
            p_w = jnp.exp(s_w - m)
            p_c = jnp.exp(s_c - m)
            l = jnp.sum(p_w, axis=-1, keepdims=True) + jnp.sum(p_c, axis=-1, keepdims=True)
            o = _dot(p_w.astype(BF16), vw) + _dot(p_c.astype(BF16), vc)
            outs.append(o / l)
        o_ref[0, :, cols] = jnp.where(low, outs[0], outs[1]).astype(BF16)


def _na_bias_table(rpb):
    qc = np.arange(GRID_W)[:, None]
    kc = np.arange(GRID_W)[None, :]
    cstart = np.clip(qc - NA_KW // 2, 0, GRID_W - NA_KW)
    valid = (kc >= cstart) & (kc < cstart + NA_KW)
    dc = np.clip(kc - qc + NA_KW - 1, 0, 2 * NA_KW - 2)
    dr = np.arange(NA_KH)[:, None] + np.arange(NA_KH)[None, :]
    t = rpb.astype(F32)[:, dr][:, :, :, dc]
    t = jnp.where(valid[None, None, None], t, NEG_INF)
    return t.transpose(1, 0, 3, 2, 4).reshape(NA_KH, NA_HEADS, GRID_W, NA_WIN)


def _na(qkv_x, qkv_c, bias_tab):
    b_ = qkv_x.shape[0]

    def d0(r):
        return _na_row_start(r) - r + NA_KH - 1

    return pl.pallas_call(
        _na_body,
        grid=(b_, ROWS),
        in_specs=[pl.BlockSpec((1, GRID_W, NA_WIDTH), lambda b, r: (b, r, 0)),
                  pl.BlockSpec((1, SEQ, NA_WIDTH), lambda b, r: (b, 0, 1)),
                  pl.BlockSpec((1, SEQ, NA_WIDTH), lambda b, r: (b, 0, 2)),
                  pl.BlockSpec((1, CTX_LEN, NA_WIDTH), lambda b, r: (b, 0, 1)),
                  pl.BlockSpec((1, CTX_LEN, NA_WIDTH), lambda b, r: (b, 0, 2)),
                  pl.BlockSpec((1, NA_HEADS, GRID_W, NA_WIN), lambda b, r: (d0(r), 0, 0, 0))],
        out_specs=pl.BlockSpec((1, GRID_W, NA_WIDTH), lambda b, r: (b, r, 0)),
        out_shape=jax.ShapeDtypeStruct((b_, SEQ, NA_WIDTH), BF16),
        compiler_params=_cparams("parallel", "arbitrary"),
        name="na",
    )(qkv_x, qkv_x, qkv_x, qkv_c, qkv_c, bias_tab)


def _hg_gates(z, lbv):
    f = lbv + (1.0 - lbv) * jax.nn.sigmoid(z)
    return jnp.log(f), 1.0 - f


def _hg_cumsum(lf, tri):
    hi = lf.astype(BF16)
    r1 = lf - hi.astype(F32)
    mid = r1.astype(BF16)
    lo = (r1 - mid.astype(F32)).astype(BF16)
    return _dot(tri, hi) + _dot(tri, mid) + _dot(tri, lo)


def _hg_tri(rev):
    t = lax.broadcasted_iota(jnp.int32, (HG_CHUNK, HG_CHUNK), 0)
    s = lax.broadcasted_iota(jnp.int32, (HG_CHUNK, HG_CHUNK), 1)
    return jnp.where((s >= t) if rev else (s <= t), 1.0, 0.0).astype(BF16)


def _hg_state_update(st, k, v, b, rev):
    b_end = b[0:1] if rev else b[HG_CHUNK - 1:HG_CHUNK]
    kdec = k * jnp.exp(b_end - b)
    return st * jnp.exp(b_end) + _dot_tn(v.astype(BF16), kdec.astype(BF16))


def _hg_group_row(a, s):
    a4 = a.reshape(HG_CHUNK // HG_SUB, HG_SUB, HG_DK)
    return jnp.broadcast_to(a4[:, s:s + 1, :], a4.shape).reshape(HG_CHUNK, HG_DK)


def _hg_intra(qa, k, v, b, rev):
    t = lax.broadcasted_iota(jnp.int32, (HG_CHUNK, HG_DK), 0)
    tl = t % HG_SUB
    nsub = HG_CHUNK // HG_SUB
    a_off = jnp.zeros((HG_CHUNK, HG_CHUNK), F32)
    for j in (range(1, nsub) if rev else range(nsub - 1)):
        edge = j * HG_SUB if rev else j * HG_SUB + HG_SUB - 1
        e = b[edge:edge + 1]
        qmask = (t < j * HG_SUB) if rev else (t >= (j + 1) * HG_SUB)
        kmask = (t >= j * HG_SUB) & (t < (j + 1) * HG_SUB)
        qd = qa * jnp.exp(jnp.where(qmask, b - e, NEG_INF))
        kd = k * jnp.exp(jnp.where(kmask, e - b, NEG_INF))
        a_off = a_off + _dot_nt(qd.astype(BF16), kd.astype(BF16))
    ps = []
    for s in range(HG_SUB):
        mask = (tl <= s) if rev else (tl >= s)
        dec = jnp.exp(jnp.where(mask, b - _hg_group_row(b, s), NEG_INF))
        ps.append((qa * _hg_group_row(k, s) * dec).astype(BF16))
    p = jnp.concatenate(ps, axis=0)
    rs = _dot(p, jnp.ones((HG_DK, HG_DK), BF16))
    lane = lax.broadcasted_iota(jnp.int32, (HG_CHUNK, HG_DK), 1)
    a_diag = jnp.zeros((HG_CHUNK, HG_DK), F32)
    for s in range(HG_SUB):
        a_diag = a_diag + jnp.where(lane == (t - tl) + s, rs[s * HG_CHUNK:(s + 1) * HG_CHUNK], 0.0)
    a = a_off + a_diag[:, 0:HG_CHUNK]
    return _dot(a.astype(BF16), v.astype(BF16))


def _hgrn_body(q_ref, ff_ref, fb_ref, i_ref, g_ref, cff_ref, cfb_ref, ci_ref, lb_ref, ng_ref, o_ref,
               accf_ref, accb_ref):
    lbr = lb_ref[...]
    mx = jnp.maximum(lbr[0], lbr[1])
    e0 = jnp.exp(lbr[0] - mx)
    lb_dirs = e0 / (e0 + jnp.exp(lbr[1] - mx))
    n_ctx = CTX_LEN // HG_CHUNK
    n_x = SEQ // HG_CHUNK
    lbvs = (lb_dirs[0:1], lb_dirs[1:2])
    tris = (_hg_tri(False), _hg_tri(True))
    zc_refs = (cff_ref, cfb_ref)
    zx_refs = (ff_ref, fb_ref)
    acc_refs = (accf_ref, accb_ref)

    def chunk_rows(n, count, rev):
        c = (count - 1 - n) if rev else n
        return pl.ds(pl.multiple_of(c * HG_CHUNK, HG_CHUNK), HG_CHUNK)

    def ctx_step(n, sts):
        out = []
        for d in range(2):
            rows = chunk_rows(n, n_ctx, d == 1)
            lf, k = _hg_gates(zc_refs[d][0, rows, :], lbvs[d])
            out.append(_hg_state_update(sts[d], k, ci_ref[0, rows, :], _hg_cumsum(lf, tris[d]), d == 1))
        return tuple(out)

    def x_step(n, sts):
        out = []
        for d in range(2):
            rev = d == 1
            rows = chunk_rows(n, n_x, rev)
            lf, k = _hg_gates(zx_refs[d][0, rows, :], lbvs[d])
            v = i_ref[0, rows, :]
            qa = _silu(q_ref[0, rows, :])
            b = _hg_cumsum(lf, tris[d])
            acc_refs[d][rows, :] = (_dot_nt((qa * jnp.exp(b)).astype(BF16), sts[d].astype(BF16))
                                    + _hg_intra(qa, k, v, b, rev))
            out.append(_hg_state_update(sts[d], k, v, b, rev))
        return tuple(out)

    zero = jnp.zeros((HG_DK, HG_DK), F32)
    sts = lax.fori_loop(0, n_ctx, ctx_step, (zero, zero))
    lax.fori_loop(0, n_x, x_step, sts)

    ng = ng_ref[...]

    def fin(n, carry):
        rows = pl.ds(pl.multiple_of(n * 256, 256), 256)
        o = accf_ref[rows, :] + accb_ref[rows, :]
        y = o * lax.rsqrt(jnp.mean(o * o, axis=-1, keepdims=True) + EPS) * ng
        o_ref[0, rows, :] = (y * _silu(g_ref[0, rows, :])).astype(BF16)
        return carry

    lax.fori_loop(0, SEQ // 256, fin, 0)


def _hgrn(hg_x, hg_c, hg_lb, hg_norm):
    b_ = hg_x.shape[0]

    def xs(j):
        return pl.BlockSpec((1, SEQ, HG_DK), lambda b, h: (b, 0, j * HG_HEADS + h))

    def cs(j):
        return pl.BlockSpec((1, CTX_LEN, HG_DK), lambda b, h: (b, 0, j * HG_HEADS + h))

    return pl.pallas_call(
        _hgrn_body,
        grid=(b_, HG_HEADS),
        in_specs=[xs(0), xs(1), xs(2), xs(3), xs(4), cs(1), cs(2), cs(3),
                  pl.BlockSpec((2, 2, HG_DK), lambda b, h: (0, 0, h)),
                  pl.BlockSpec((1, HG_DK), lambda b, h: (0, h))],
        out_specs=pl.BlockSpec((1, SEQ, HG_DK), lambda b, h: (b, 0, h)),
        out_shape=jax.ShapeDtypeStruct((b_, SEQ, HG_WIDTH), BF16),
        scratch_shapes=[pltpu.VMEM((SEQ, HG_DK), F32), pltpu.VMEM((SEQ, HG_DK), F32)],
        compiler_params=_cparams("parallel", "parallel"),
        name="hgrn",
    )(hg_x, hg_x, hg_x, hg_x, hg_x, hg_c, hg_c, hg_c, hg_lb, hg_norm)


def _mix_body(na_ref, hg_ref, x_ref, gt_ref, g2_ref, sh_ref, sc_ref, wo_ref, wq_ref, k1_ref, k2_ref, k2h_ref,
              x1_ref, h2_ref, s1_ref, s2_ref, s2h_ref):
    mix = _dot(na_ref[0], wo_ref[0:NA_WIDTH, :]) + _dot(hg_ref[0], wo_ref[NA_WIDTH:, :])
    x1 = x_ref[0] + gt_ref[0] * mix
    x1_ref[0] = x1
    h2 = _norm_mod(x1, g2_ref[...], sh_ref[0], sc_ref[0]).astype(BF16)
    h2_ref[0] = h2
    q = _dot(h2, wq_ref[...]).astype(BF16)
    q1 = q[:, 0:PEER_QW]
    q2 = q[:, PEER_QW:]
    s1_ref[0] = _dot_nt(k1_ref[...], q1)
    s2_ref[0] = _dot_nt(k2_ref[...], q2)
    s2h_ref[0] = _dot_nt(k2h_ref[...], q2)


def _mix(na, hg, x, mod3, g2, w_out, wq, k1, k2, k2h, tm):
    b_ = x.shape[0]
    tok = lambda b, i: (b, i, 0)
    tokt = lambda b, i: (b, 0, i)
    full = lambda b, i: (0, 0)
    return pl.pallas_call(
        _mix_body,
        grid=(b_, SEQ // tm),
        in_specs=[pl.BlockSpec((1, tm, NA_WIDTH), tok),
                  pl.BlockSpec((1, tm, HG_WIDTH), tok),
                  pl.BlockSpec((1, tm, D_MODEL), tok),
                  pl.BlockSpec((1, 1, D_MODEL), lambda b, i: (b, 0, 2)),
                  pl.BlockSpec((1, D_MODEL), full),
                  pl.BlockSpec((1, 1, D_MODEL), lambda b, i: (b, 0, 3)),
                  pl.BlockSpec((1, 1, D_MODEL), lambda b, i: (b, 0, 4)),
                  pl.BlockSpec((D_MODEL, D_MODEL), full),
                  pl.BlockSpec((D_MODEL, 2 * PEER_QW), full),
                  pl.BlockSpec((PEER_QW, PEER_QW), full),
                  pl.BlockSpec((PEER_QW, PEER_QW), full),
                  pl.BlockSpec((PEER_QW, PEER_QW), full)],
        out_specs=[pl.BlockSpec((1, tm, D_MODEL), tok),
                   pl.BlockSpec((1, tm, D_MODEL), tok),
                   pl.BlockSpec((1, PEER_QW, tm), tokt),
                   pl.BlockSpec((1, PEER_QW, tm), tokt),
                   pl.BlockSpec((1, PEER_QW, tm), tokt)],
        out_shape=[jax.ShapeDtypeStruct((b_, SEQ, D_MODEL), F32),
                   jax.ShapeDtypeStruct((b_, SEQ, D_MODEL), BF16),
                   jax.ShapeDtypeStruct((b_, PEER_QW, SEQ), F32),
                   jax.ShapeDtypeStruct((b_, PEER_QW, SEQ), F32),
                   jax.ShapeDtypeStruct((b_, PEER_QW, SEQ), F32)],
        compiler_params=_cparams("parallel", "parallel"),
        name="mix",
    )(na, hg, x, mod3, g2, mod3, mod3, w_out, wq, k1, k2, k2h)


def _top_values(s, n):
    vals = [jnp.max(s, axis=0)]
    for _ in range(n - 1):
        vals.append(jnp.max(jnp.where(s < vals[-1][None], s, NEG_INF), axis=0))
    return vals


def _route_body(s1_ref, s2_ref, s2h_ref, n_ref, e1_ref, r2_ref, e2_ref):
    tl = s1_ref.shape[-1]
    s1 = s1_ref[0].reshape(PEER_NKEYS, PEER_HEADS, tl)
    s2 = s2_ref[0].reshape(PEER_NKEYS, PEER_HEADS, tl)
    n = PEER_TOPK + 1
    a = _top_values(s1, n)
    b = _top_values(s2, n)
    cands = jnp.stack([a[k] + b[l] for k in range(n) for l in range(n) if (k + 1) * (l + 1) <= n], axis=0)
    tops = _top_values(cands, n)
    thr = tops[PEER_TOPK - 1]
    cut = 0.5 * (thr + tops[PEER_TOPK])
    top = a[0] + b[0]
    z = jnp.sum(jnp.where(cands >= thr[None], jnp.exp(cands - top[None]), 0.0), axis=0)
    need = cut[None] - s1
    cnt = jnp.zeros_like(s1)
    for l in range(n):
        cnt = cnt + jnp.where(b[l][None] >= need, 1.0, 0.0)
    n_ref[0] = cnt.reshape(PEER_QW, tl)
    e1_ref[0] = (jnp.exp(s1 - a[0][None]) / z[None]).reshape(PEER_QW, tl)
    for p in range(PEER_HEADS):
        rows = slice(p * PEER_NKEYS, (p + 1) * PEER_NKEYS)
        s2p = s2h_ref[0, rows, :]
        rank = jnp.zeros_like(s2p)
        for l in range(n):
            rank = rank + jnp.where(b[l][p:p + 1] > s2p, 1.0, 0.0)
        r2_ref[0, rows, :] = rank.astype(BF16)
        e2_ref[0, rows, :] = jnp.exp(s2p - b[0][p:p + 1]).astype(BF16)


def _route(s1, s2, s2h, tl):
    b_ = s1.shape[0]
    spec = pl.BlockSpec((1, PEER_QW, tl), lambda b, i: (b, 0, i))
    f32 = jax.ShapeDtypeStruct((b_, PEER_QW, SEQ), F32)
    bf16 = jax.ShapeDtypeStruct((b_, PEER_QW, SEQ), BF16)
    return pl.pallas_call(
        _route_body,
        grid=(b_, SEQ // tl),
        in_specs=[spec, spec, spec],
        out_specs=[spec, spec, spec, spec],
        out_shape=[f32, f32, bf16, bf16],
        compiler_params=_cparams("parallel", "parallel"),
        name="route",
    )(s1, s2, s2h)


PEER_EC = 1024


def _gelu(a):
    return 0.5 * a * (1.0 + lax.erf(a * 0.7071067811865476))


BF16_ROWS = 16


def _peer_body(h2_ref, u_ref, vt_ref, n_ref, e1_ref, r2_ref, e2_ref, x1_ref, gt_ref, gf_ref, y_ref,
               acc_ref, w_ref):
    e = pl.program_id(2)
    tm = h2_ref.shape[1]
    packed = (PEER_NKEYS // BF16_ROWS, BF16_ROWS, tm)

    @pl.when(e == 0)
    def _():
        acc_ref[...] = jnp.zeros_like(acc_ref)

    at = _dot_nt(u_ref[...], h2_ref[0])
    for il in range(PEER_EC // PEER_NKEYS):
        g = None
        for p in range(PEER_HEADS):
            row = il * PEER_HEADS + p
            keys = slice(p * PEER_NKEYS, (p + 1) * PEER_NKEYS)
            cnt = jnp.broadcast_to(n_ref[0, row:row + 1, :], (BF16_ROWS, tm)).astype(BF16)
            e1 = jnp.broadcast_to(e1_ref[0, row:row + 1, :], (BF16_ROWS, tm)).astype(BF16)
            sel = r2_ref[0, keys, :].reshape(packed) < cnt[None]
            gp = jnp.where(sel, e2_ref[0, keys, :].reshape(packed) * e1[None], jnp.zeros(packed, BF16))
            g = gp if g is None else g + gp
        rows = slice(il * PEER_NKEYS, (il + 1) * PEER_NKEYS)
        w_ref[rows, :] = g.reshape(PEER_NKEYS, tm) * _gelu(at[rows, :]).astype(BF16)
    acc_ref[...] += _dot(vt_ref[...], w_ref[...])

    @pl.when(e == pl.num_programs(2) - 1)
    def _():
        xo = x1_ref[0] + gt_ref[0] * acc_ref[...].T
        y_ref[0] = xo * lax.rsqrt(jnp.mean(xo * xo, axis=-1, keepdims=True) + EPS) * gf_ref[...]


def _peer(h2, u, vt, cnt, e1, r2, e2, x1, mod3, gf, tm):
    b_ = h2.shape[0]
    ne = PEER_EXPERTS // PEER_EC
    rows_per = PEER_EC // PEER_NKEYS * PEER_HEADS
    tok = lambda b, i, e: (b, i, 0)
    tokt = lambda b, i, e: (b, 0, i)
    return pl.pallas_call(
        _peer_body,
        grid=(b_, SEQ // tm, ne),
        in_specs=[pl.BlockSpec((1, tm, D_MODEL), tok),
                  pl.BlockSpec((PEER_EC, D_MODEL), lambda b, i, e: (e, 0)),
                  pl.BlockSpec((D_MODEL, PEER_EC), lambda b, i, e: (0, e)),
                  pl.BlockSpec((1, rows_per, tm), lambda b, i, e: (b, e, i)),
                  pl.BlockSpec((1, rows_per, tm), lambda b, i, e: (b, e, i)),
                  pl.BlockSpec((1, PEER_QW, tm), tokt),
                  pl.BlockSpec((1, PEER_QW, tm), tokt),
                  pl.BlockSpec((1, tm, D_MODEL), tok),
                  pl.BlockSpec((1, 1, D_MODEL), lambda b, i, e: (b, 0, 5)),
                  pl.BlockSpec((1, D_MODEL), lambda b, i, e: (0, 0))],
        out_specs=pl.BlockSpec((1, tm, D_MODEL), tok),
        out_shape=jax.ShapeDtypeStruct((b_, SEQ, D_MODEL), F32),
        scratch_shapes=[pltpu.VMEM((D_MODEL, tm), F32), pltpu.VMEM((PEER_EC, tm), BF16)],
        compiler_params=_cparams("parallel", "parallel", "arbitrary"),
        name="peer",
    )(h2, u, vt, cnt, e1, r2, e2, x1, mod3, gf)


def _peer_key_layouts(keys):
    ar = np.arange(PEER_HEADS)
    out = []
    for c, key_major in ((0, True), (1, True), (1, False)):
        kc = keys[:, c].astype(BF16)
        if key_major:
            m = jnp.zeros((PEER_NKEYS, PEER_HEADS, PEER_HEADS, PEER_HALF), BF16)
            m = m.at[:, ar, ar, :].set(kc.transpose(1, 0, 2))
        else:
            m = jnp.zeros((PEER_HEADS, PEER_NKEYS, PEER_HEADS, PEER_HALF), BF16)
            m = m.at[ar, :, ar, :].set(kc)
        out.append(m.reshape(PEER_QW, PEER_QW))
    return out


def kernel(x, c, ctx, c_ctx, w_mod, b_mod, norm1, norm2, w_in, w_out, na_rpb, hg_lb, hg_norm,
           peer_wq, peer_keys, peer_u, peer_v, norm_f):
    assert w_mod.shape[0] == 1 and hg_lb.shape[0] == 2, "single-layer configuration"
    b_ = x.shape[0]
    pad = (-(b_ + 1)) % 8
    c_rows = jnp.concatenate([c, c_ctx[None], jnp.zeros((pad, D_MODEL), F32)], axis=0)
    mod = _mod(c_rows, w_mod[0], b_mod[0][None])
    mod3 = mod.reshape(mod.shape[0], 1, N_MOD * D_MODEL)

    w_in_b = w_in[0].astype(BF16)
    g1 = norm1[0][None]
    qkv_x, hg_x = _inproj(x, mod3, lambda b: b, g1, w_in_b, 512)
    qkv_c, hg_c = _inproj(ctx, mod3, lambda b: b_, g1, w_in_b, CTX_LEN)

    na = _na(qkv_x, qkv_c, _na_bias_table(na_rpb[0]))
    hg = _hgrn(hg_x, hg_c, hg_lb, hg_norm[0][None])

    wq = peer_wq[0].reshape(D_MODEL, PEER_HEADS, 2, PEER_HALF).transpose(0, 2, 1, 3)
    wq = wq.reshape(D_MODEL, 2 * PEER_QW).astype(BF16)
    k1, k2, k2h = _peer_key_layouts(peer_keys[0])
    x1, h2, s1, s2, s2h = _mix(na, hg, x, mod3, norm2[0][None], w_out[0].astype(BF16), wq, k1, k2, k2h, 256)

    cnt, e1, r2, e2 = _route(s1, s2, s2h, 256)
    u = peer_u[0].astype(BF16)
    vt = peer_v[0].T.astype(BF16)
    return _peer(h2, u, vt, cnt, e1, r2, e2, x1, mod3, norm_f[None], 512)
```

```python
import functools

import numpy as np
import jax
import jax.numpy as jnp
from jax import lax
from jax.experimental import pallas as pl
from jax.experimental.pallas import tpu as pltpu

F32 = jnp.float32
BF16 = jnp.bfloat16

D_MODEL = 1024
SEQ = 2048
CTX_LEN = 256
GRID_W = 64
ROWS = SEQ // GRID_W

NA_HEADS = 8
NA_HEAD_DIM = 64
NA_WIDTH = NA_HEADS * NA_HEAD_DIM
NA_KH = 8
NA_KW = 16
NA_WIN = NA_KH * GRID_W

HG_HEADS = 4
HG_DK = 128
HG_WIDTH = HG_HEADS * HG_DK
HG_CHUNK = 64
HG_SUB = 16

IN_COLS = 3 * NA_WIDTH + 5 * HG_WIDTH

PEER_HEADS = 8
PEER_NKEYS = 128
PEER_EXPERTS = PEER_NKEYS * PEER_NKEYS
PEER_TOPK = 16
PEER_HALF = 128
PEER_QW = PEER_HEADS * PEER_HALF

N_MOD = 6
EPS = 1e-6
NEG_INF = float("-inf")

LANES = 128
VMEM_LIMIT = 56 * 1024 * 1024


def _cparams(*sem, flags=None):
    return pltpu.CompilerParams(dimension_semantics=sem, vmem_limit_bytes=VMEM_LIMIT, flags=flags)


def _silu(x):
    return x * jax.nn.sigmoid(x)


def _dot(a, b):
    return jnp.dot(a, b, preferred_element_type=F32)


def _dot_nt(a, b):
    return lax.dot_general(a, b, (((1,), (1,)), ((), ())), preferred_element_type=F32)


def _dot_tn(a, b):
    return lax.dot_general(a, b, (((0,), (0,)), ((), ())), preferred_element_type=F32)


def _mod_body(c_ref, w_ref, b_ref, o_ref):
    o_ref[...] = _dot(_silu(c_ref[...]).astype(BF16), w_ref[...].astype(BF16)) + b_ref[...]


def _mod(c_rows, w_mod, b_mod):
    n = c_rows.shape[0]
    return pl.pallas_call(
        _mod_body,
        grid=(N_MOD,),
        in_specs=[pl.BlockSpec((n, D_MODEL), lambda j: (0, 0)),
                  pl.BlockSpec((D_MODEL, D_MODEL), lambda j: (0, j)),
                  pl.BlockSpec((1, D_MODEL), lambda j: (0, j))],
        out_specs=pl.BlockSpec((n, D_MODEL), lambda j: (0, j)),
        out_shape=jax.ShapeDtypeStruct((n, N_MOD * D_MODEL), F32),
        compiler_params=_cparams("arbitrary"),
        name="mod",
    )(c_rows, w_mod, b_mod)


def _norm_mod(x, g, shift, scale):
    y = x * lax.rsqrt(jnp.mean(x * x, axis=-1, keepdims=True) + EPS) * g
    return y * (1.0 + scale) + shift


def _inproj_body(x_ref, sh_ref, sc_ref, g_ref, w_ref, qkv_ref, hg_ref):
    h = _norm_mod(x_ref[0], g_ref[...], sh_ref[0], sc_ref[0]).astype(BF16)
    p = _dot(h, w_ref[:, 0:1024])
    qkv_ref[0, :, 0:512] = (p[:, 0:512] * (NA_HEAD_DIM ** -0.5)).astype(BF16)
    qkv_ref[0, :, 512:1024] = p[:, 512:1024].astype(BF16)
    p = _dot(h, w_ref[:, 1024:2048])
    qkv_ref[0, :, 1024:1536] = p[:, 0:512].astype(BF16)
    hg_ref[0, :, 0:512] = p[:, 512:1024]
    hg_ref[0, :, 512:1536] = _dot(h, w_ref[:, 2048:3072])
    hg_ref[0, :, 1536:2560] = _dot(h, w_ref[:, 3072:4096])


def _inproj(x, mod3, mod_row, g, w_in, tm):
    b_, t_, _ = x.shape
    return pl.pallas_call(
        _inproj_body,
        grid=(b_, t_ // tm),
        in_specs=[pl.BlockSpec((1, tm, D_MODEL), lambda b, i: (b, i, 0)),
                  pl.BlockSpec((1, 1, D_MODEL), lambda b, i: (mod_row(b), 0, 0)),
                  pl.BlockSpec((1, 1, D_MODEL), lambda b, i: (mod_row(b), 0, 1)),
                  pl.BlockSpec((1, D_MODEL), lambda b, i: (0, 0)),
                  pl.BlockSpec((D_MODEL, IN_COLS), lambda b, i: (0, 0))],
        out_specs=[pl.BlockSpec((1, tm, 3 * NA_WIDTH), lambda b, i: (b, i, 0)),
                   pl.BlockSpec((1, tm, 5 * HG_WIDTH), lambda b, i: (b, i, 0))],
        out_shape=[jax.ShapeDtypeStruct((b_, t_, 3 * NA_WIDTH), BF16),
                   jax.ShapeDtypeStruct((b_, t_, 5 * HG_WIDTH), F32)],
        compiler_params=_cparams("parallel", "parallel"),
        name="inproj",
    )(x, mod3, mod3, g, w_in)


def _na_row_start(r):
    return jnp.clip(r - NA_KH // 2, 0, ROWS - NA_KH)


def _na_body(q_ref, k_ref, v_ref, kc_ref, vc_ref, bias_ref, o_ref):
    r = pl.program_id(1)
    start = pl.multiple_of(_na_row_start(r) * GRID_W, GRID_W)
    lane = lax.broadcasted_iota(jnp.int32, (GRID_W, 2 * NA_HEAD_DIM), 1)
    low = lane < NA_HEAD_DIM
    s_w, s_c = [], []
    for h in range(NA_HEADS):
        cols = slice(h // 2 * 128, (h // 2 + 1) * 128)
        q2 = q_ref[0, :, cols]
        qh = jnp.where(low if h % 2 == 0 else ~low, q2, jnp.zeros_like(q2))
        s_w.append(_dot_nt(qh, k_ref[0, pl.ds(start, NA_WIN), cols]))
        s_c.append(_dot_nt(qh, kc_ref[0, :, cols]))
    s_w = jnp.concatenate(s_w, axis=0) + bias_ref[0].reshape(NA_HEADS * GRID_W, NA_WIN)
    s_c = jnp.concatenate(s_c, axis=0)
    m = jnp.maximum(jnp.max(s_w, axis=-1, keepdims=True), jnp.max(s_c, axis=-1, keepdims=True))
    p_w = jnp.exp(s_w - m)
    p_c = jnp.exp(s_c - m)
    inv_l = 1.0 / (jnp.sum(p_w, axis=-1, keepdims=True) + jnp.sum(p_c, axis=-1, keepdims=True))
    p_w = p_w.astype(BF16)
    p_c = p_c.astype(BF16)
    for hp in range(NA_HEADS // 2):
        cols = slice(hp * 128, (hp + 1) * 128)
        vw = v_ref[0, pl.ds(start, NA_WIN), cols]
        vc = vc_ref[0, :, cols]
        outs = []
        for h in (2 * hp, 2 * hp + 1):
            rows = slice(h * GRID_W, (h + 1) * GRID_W)
            o = _dot(p_w[rows], vw) + _dot(p_c[rows], vc)
            outs.append(o * inv_l[rows])
        o_ref[0, :, cols] = jnp.where(low, outs[0], outs[1]).astype(BF16)


def _na_bias_table(rpb):
    qc = np.arange(GRID_W)[:, None]
    kc = np.arange(GRID_W)[None, :]
    cstart = np.clip(qc - NA_KW // 2, 0, GRID_W - NA_KW)
    valid = (kc >= cstart) & (kc < cstart + NA_KW)
    dc = np.clip(kc - qc + NA_KW - 1, 0, 2 * NA_KW - 2)
    dr = np.arange(NA_KH)[:, None] + np.arange(NA_KH)[None, :]
    t = rpb.astype(F32)[:, dr][:, :, :, dc]
    t = jnp.where(valid[None, None, None], t, NEG_INF)
    return t.transpose(1, 0, 3, 2, 4).reshape(NA_KH, NA_HEADS, GRID_W, NA_WIN)


def _na(qkv_x, qkv_c, bias_tab):
    b_ = qkv_x.shape[0]

    def d0(r):
        return _na_row_start(r) - r + NA_KH - 1

    return pl.pallas_call(
        _na_body,
        grid=(b_, ROWS),
        in_specs=[pl.BlockSpec((1, GRID_W, NA_WIDTH), lambda b, r: (b, r, 0)),
                  pl.BlockSpec((1, SEQ, NA_WIDTH), lambda b, r: (b, 0, 1)),
                  pl.BlockSpec((1, SEQ, NA_WIDTH), lambda b, r: (b, 0, 2)),
                  pl.BlockSpec((1, CTX_LEN, NA_WIDTH), lambda b, r: (b, 0, 1)),
                  pl.BlockSpec((1, CTX_LEN, NA_WIDTH), lambda b, r: (b, 0, 2)),
                  pl.BlockSpec((1, NA_HEADS, GRID_W, NA_WIN), lambda b, r: (d0(r), 0, 0, 0))],
        out_specs=pl.BlockSpec((1, GRID_W, NA_WIDTH), lambda b, r: (b, r, 0)),
        out_shape=jax.ShapeDtypeStruct((b_, SEQ, NA_WIDTH), BF16),
        compiler_params=_cparams("parallel", "arbitrary"),
        name="na",
    )(qkv_x, qkv_x, qkv_x, qkv_c, qkv_c, bias_tab)


def _hg_gates(z, lbv):
    f = lbv + (1.0 - lbv) * jax.nn.sigmoid(z)
    return jnp.log(f), 1.0 - f


def _hg_cumsum(lf, tri):
    hi = lf.astype(BF16)
    r1 = lf - hi.astype(F32)
    mid = r1.astype(BF16)
    lo = (r1 - mid.astype(F32)).astype(BF16)
    return _dot(tri, hi) + _dot(tri, mid) + _dot(tri, lo)


def _hg_tri(rev):
    t = lax.broadcasted_iota(jnp.int32, (HG_CHUNK, HG_CHUNK), 0)
    s = lax.broadcasted_iota(jnp.int32, (HG_CHUNK, HG_CHUNK), 1)
    return jnp.where((s >= t) if rev else (s <= t), 1.0, 0.0).astype(BF16)


def _hg_state_update(st, k, v, b, rev):
    b_end = b[0:1] if rev else b[HG_CHUNK - 1:HG_CHUNK]
    kdec = k * jnp.exp(b_end - b)
    return st * jnp.exp(b_end) + _dot_tn(v.astype(BF16), kdec.astype(BF16))


def _hg_group_row(a, s):
    a4 = a.reshape(HG_CHUNK // HG_SUB, HG_SUB, HG_DK)
    return jnp.broadcast_to(a4[:, s:s + 1, :], a4.shape).reshape(HG_CHUNK, HG_DK)


def _hg_intra(qa, k, v, b, rev):
    t = lax.broadcasted_iota(jnp.int32, (HG_CHUNK, HG_DK), 0)
    tl = t % HG_SUB
    nsub = HG_CHUNK // HG_SUB
    a_off = jnp.zeros((HG_CHUNK, HG_CHUNK), F32)
    for j in (range(1, nsub) if rev else range(nsub - 1)):
        edge = j * HG_SUB if rev else j * HG_SUB + HG_SUB - 1
        e = b[edge:edge + 1]
        qmask = (t < j * HG_SUB) if rev else (t >= (j + 1) * HG_SUB)
        kmask = (t >= j * HG_SUB) & (t < (j + 1) * HG_SUB)
        qd = qa * jnp.exp(jnp.where(qmask, b - e, NEG_INF))
        kd = k * jnp.exp(jnp.where(kmask, e - b, NEG_INF))
        a_off = a_off + _dot_nt(qd.astype(BF16), kd.astype(BF16))
    ps = []
    for s in range(HG_SUB):
        mask = (tl <= s) if rev else (tl >= s)
        dec = jnp.exp(jnp.where(mask, b - _hg_group_row(b, s), NEG_INF))
        ps.append((qa * _hg_group_row(k, s) * dec).astype(BF16))
    p = jnp.concatenate(ps, axis=0)
    rs = _dot(p, jnp.ones((HG_DK, HG_DK), BF16))
    lane = lax.broadcasted_iota(jnp.int32, (HG_CHUNK, HG_DK), 1)
    a_diag = jnp.zeros((HG_CHUNK, HG_DK), F32)
    for s in range(HG_SUB):
        a_diag = a_diag + jnp.where(lane == (t - tl) + s, rs[s * HG_CHUNK:(s + 1) * HG_CHUNK], 0.0)
    a = a_off + a_diag[:, 0:HG_CHUNK]
    return _dot(a.astype(BF16), v.astype(BF16))


def _hgrn_body(q_ref, ff_ref, fb_ref, i_ref, g_ref, cff_ref, cfb_ref, ci_ref, lb_ref, ng_ref, o_ref,
               accf_ref, accb_ref):
    lbr = lb_ref[...]
    mx = jnp.maximum(lbr[0], lbr[1])
    e0 = jnp.exp(lbr[0] - mx)
    lb_dirs = e0 / (e0 + jnp.exp(lbr[1] - mx))
    n_ctx = CTX_LEN // HG_CHUNK
    n_x = SEQ // HG_CHUNK
    lbvs = (lb_dirs[0:1], lb_dirs[1:2])
    tris = (_hg_tri(False), _hg_tri(True))
    zc_refs = (cff_ref, cfb_ref)
    zx_refs = (ff_ref, fb_ref)
    acc_refs = (accf_ref, accb_ref)

    def chunk_rows(n, count, rev):
        c = (count - 1 - n) if rev else n
        return pl.ds(pl.multiple_of(c * HG_CHUNK, HG_CHUNK), HG_CHUNK)

    def ctx_step(n, sts):
        out = []
        for d in range(2):
            rows = chunk_rows(n, n_ctx, d == 1)
            lf, k = _hg_gates(zc_refs[d][0, rows, :], lbvs[d])
            out.append(_hg_state_update(sts[d], k, ci_ref[0, rows, :], _hg_cumsum(lf, tris[d]), d == 1))
        return tuple(out)

    def x_step(n, sts):
        out = []
        for d in range(2):
            rev = d == 1
            rows = chunk_rows(n, n_x, rev)
            lf, k = _hg_gates(zx_refs[d][0, rows, :], lbvs[d])
            v = i_ref[0, rows, :]
            qa = _silu(q_ref[0, rows, :])
            b = _hg_cumsum(lf, tris[d])
            acc_refs[d][rows, :] = (_dot_nt((qa * jnp.exp(b)).astype(BF16), sts[d].astype(BF16))
                                    + _hg_intra(qa, k, v, b, rev))
            out.append(_hg_state_update(sts[d], k, v, b, rev))
        return tuple(out)

    zero = jnp.zeros((HG_DK, HG_DK), F32)
    sts = lax.fori_loop(0, n_ctx, ctx_step, (zero, zero))
    lax.fori_loop(0, n_x, x_step, sts)

    ng = ng_ref[...]

    def fin(n, carry):
        rows = pl.ds(pl.multiple_of(n * 256, 256), 256)
        o = accf_ref[rows, :] + accb_ref[rows, :]
        y = o * lax.rsqrt(jnp.mean(o * o, axis=-1, keepdims=True) + EPS) * ng
        o_ref[0, rows, :] = (y * _silu(g_ref[0, rows, :])).astype(BF16)
        return carry

    lax.fori_loop(0, SEQ // 256, fin, 0)


def _hgrn(hg_x, hg_c, hg_lb, hg_norm):
    b_ = hg_x.shape[0]

    def xs(j):
        return pl.BlockSpec((1, SEQ, HG_DK), lambda b, h: (b, 0, j * HG_HEADS + h))

    def cs(j):
        return pl.BlockSpec((1, CTX_LEN, HG_DK), lambda b, h: (b, 0, j * HG_HEADS + h))

    return pl.pallas_call(
        _hgrn_body,
        grid=(b_, HG_HEADS),
        in_specs=[xs(0), xs(1), xs(2), xs(3), xs(4), cs(1), cs(2), cs(3),
                  pl.BlockSpec((2, 2, HG_DK), lambda b, h: (0, 0, h)),
                  pl.BlockSpec((1, HG_DK), lambda b, h: (0, h))],
        out_specs=pl.BlockSpec((1, SEQ, HG_DK), lambda b, h: (b, 0, h)),
        out_shape=jax.ShapeDtypeStruct((b_, SEQ, HG_WIDTH), BF16),
        scratch_shapes=[pltpu.VMEM((SEQ, HG_DK), F32), pltpu.VMEM((SEQ, HG_DK), F32)],
        compiler_params=_cparams("parallel", "parallel"),
        name="hgrn",
    )(hg_x, hg_x, hg_x, hg_x, hg_x, hg_c, hg_c, hg_c, hg_lb, hg_norm)


def _mix_body(na_ref, hg_ref, x_ref, gt_ref, g2_ref, sh_ref, sc_ref, wo_ref, wq_ref, k1_ref, k2_ref,
              x1_ref, h2t_ref, s1_ref, s2_ref):
    mix = _dot(na_ref[0], wo_ref[0:NA_WIDTH, :]) + _dot(hg_ref[0], wo_ref[NA_WIDTH:, :])
    x1 = x_ref[0] + gt_ref[0] * mix
    x1_ref[0] = x1
    h2 = _norm_mod(x1, g2_ref[...], sh_ref[0], sc_ref[0])
    h2t_ref[0] = h2.T.astype(BF16)
    q = _dot(h2.astype(BF16), wq_ref[...]).astype(BF16)
    s1_ref[0] = _dot_nt(k1_ref[...], q[:, 0:PEER_QW])
    s2_ref[0] = _dot_nt(k2_ref[...], q[:, PEER_QW:])


def _mix(na, hg, x, mod3, g2, w_out, wq, k1, k2, tm):
    b_ = x.shape[0]
    tok = lambda b, i: (b, i, 0)
    tokt = lambda b, i: (b, 0, i)
    full = lambda b, i: (0, 0)
    return pl.pallas_call(
        _mix_body,
        grid=(b_, SEQ // tm),
        in_specs=[pl.BlockSpec((1, tm, NA_WIDTH), tok),
                  pl.BlockSpec((1, tm, HG_WIDTH), tok),
                  pl.BlockSpec((1, tm, D_MODEL), tok),
                  pl.BlockSpec((1, 1, D_MODEL), lambda b, i: (b, 0, 2)),
                  pl.BlockSpec((1, D_MODEL), full),
                  pl.BlockSpec((1, 1, D_MODEL), lambda b, i: (b, 0, 3)),
                  pl.BlockSpec((1, 1, D_MODEL), lambda b, i: (b, 0, 4)),
                  pl.BlockSpec((D_MODEL, D_MODEL), full),
                  pl.BlockSpec((D_MODEL, 2 * PEER_QW), full),
                  pl.BlockSpec((PEER_QW, PEER_QW), full),
                  pl.BlockSpec((PEER_QW, PEER_QW), full)],
        out_specs=[pl.BlockSpec((1, tm, D_MODEL), tok),
                   pl.BlockSpec((1, D_MODEL, tm), tokt),
                   pl.BlockSpec((1, PEER_QW, tm), tokt),
                   pl.BlockSpec((1, PEER_QW, tm), tokt)],
        out_shape=[jax.ShapeDtypeStruct((b_, SEQ, D_MODEL), F32),
                   jax.ShapeDtypeStruct((b_, D_MODEL, SEQ), BF16),
                   jax.ShapeDtypeStruct((b_, PEER_QW, SEQ), F32),
                   jax.ShapeDtypeStruct((b_, PEER_QW, SEQ), F32)],
        compiler_params=_cparams("parallel", "parallel"),
        name="mix",
    )(na, hg, x, mod3, g2, mod3, mod3, w_out, wq, k1, k2)


def _bitonic_merge(c):
    n = len(c)
    if n == 1:
        return c
    h = n // 2
    hi = [jnp.maximum(c[i], c[i + h]) for i in range(h)]
    lo = [jnp.minimum(c[i], c[i + h]) for i in range(h)]
    return _bitonic_merge(hi) + _bitonic_merge(lo)


def _sort_desc(xs):
    n = len(xs)
    if n == 1:
        return xs
    return _bitonic_merge(_sort_desc(xs[:n // 2]) + _sort_desc(xs[n // 2:])[::-1])


def _merge_top(a, b):
    n = len(a)
    return _bitonic_merge([jnp.maximum(a[i], b[n - 1 - i]) for i in range(n)])


def _top_sorted(xs, n):
    runs = [_sort_desc(xs[i:i + n]) for i in range(0, len(xs), n)]
    while len(runs) > 1:
        runs = [_merge_top(runs[i], runs[i + 1]) for i in range(0, len(runs), 2)]
    return runs[0]


def _count_true(pred, vals):
    assert len(vals) == PEER_TOPK
    conds = []
    total = None
    for step in (8, 4, 2, 1):
        def probe(k, lo):
            if k == len(conds):
                return vals[lo + step - 1]
            return jnp.where(conds[k], probe(k + 1, lo + (8 >> k)), probe(k + 1, lo))

        c = pred(probe(0, 0))
        conds.append(c)
        inc = jnp.where(c, float(step), 0.0)
        total = inc if total is None else total + inc
    return total + jnp.where(pred(vals[PEER_TOPK - 1]), 1.0, 0.0)


def _store_head_major(x, tmp_ref, out_ref):
    for lt in range(x.shape[-1] // LANES):
        lanes = slice(lt * LANES, (lt + 1) * LANES)
        for j in range(PEER_NKEYS):
            tmp_ref[pl.ds(j, PEER_HEADS, stride=PEER_NKEYS), :] = x[j, :, lanes]
        out_ref[0, :, lanes] = tmp_ref[...].astype(BF16)


def _route_body(s1_ref, s2_ref, n_ref, e1_ref, r2_ref, e2_ref, tmp_ref):
    tl = s1_ref.shape[-1]
    s1 = s1_ref[0].reshape(PEER_NKEYS, PEER_HEADS, tl)
    s2 = s2_ref[0].reshape(PEER_NKEYS, PEER_HEADS, tl)
    n = PEER_TOPK
    a = _top_sorted([s1[j] for j in range(PEER_NKEYS)], n)
    b = _top_sorted([s2[j] for j in range(PEER_NKEYS)], n)
    cands = [a[k] + b[l] for k in range(n) for l in range(n) if (k + 1) * (l + 1) <= n]
    cands += [jnp.full_like(cands[0], NEG_INF)] * (-len(cands) % n)
    tops = _top_sorted(cands, n)
    thr = tops[n - 1]
    z = sum(jnp.exp(t - tops[0]) for t in tops)
    cnt = _count_true(lambda v: s1 + v >= thr, b)
    rank2 = _count_true(lambda v: v > s2, b)
    n_ref[0] = cnt.reshape(PEER_QW, tl)
    e1_ref[0] = (jnp.exp(s1 - a[0][None]) / z[None]).reshape(PEER_QW, tl)
    _store_head_major(rank2, tmp_ref, r2_ref)
    _store_head_major(jnp.exp(s2 - b[0][None]), tmp_ref, e2_ref)


def _route(s1, s2, tl):
    b_ = s1.shape[0]
    spec = pl.BlockSpec((1, PEER_QW, tl), lambda b, i: (b, 0, i))
    words = jax.ShapeDtypeStruct((b_, PEER_QW, SEQ), F32)
    bf16 = jax.ShapeDtypeStruct((b_, PEER_QW, SEQ), BF16)
    return pl.pallas_call(
        _route_body,
        grid=(b_, SEQ // tl),
        in_specs=[spec, spec],
        out_specs=[spec, spec, spec, spec],
        out_shape=[words, words, bf16, bf16],
        scratch_shapes=[pltpu.VMEM((PEER_QW, LANES), F32)],
        compiler_params=_cparams("parallel", "parallel"),
        name="route",
    )(s1, s2)


PEER_EC = 1024


def _gelu(a):
    return 0.5 * a * (1.0 + lax.erf(a * 0.7071067811865476))


BF16_ROWS = 16


def _peer_body(h2t_ref, u_ref, vt_ref, n_ref, e1_ref, r2_ref, e2_ref, x1_ref, gt_ref, gf_ref, y_ref,
               acc_ref, w_ref):
    e = pl.program_id(2)
    tm = h2t_ref.shape[2]
    packed = (PEER_NKEYS // BF16_ROWS, BF16_ROWS, tm)

    @pl.when(e == 0)
    def _():
        acc_ref[...] = jnp.zeros_like(acc_ref)

    def packed_row(ref, row):
        return jnp.broadcast_to(ref[0, row:row + 1, :], (BF16_ROWS, tm)).astype(BF16)

    at = _dot(u_ref[...], h2t_ref[0])
    for il in range(PEER_EC // PEER_NKEYS):
        g = None
        for p in range(PEER_HEADS):
            row = il * PEER_HEADS + p
            keys = slice(p * PEER_NKEYS, (p + 1) * PEER_NKEYS)
            sel = r2_ref[0, keys, :].reshape(packed) < packed_row(n_ref, row)[None]
            gp = jnp.where(sel, e2_ref[0, keys, :].reshape(packed) * packed_row(e1_ref, row)[None],
                           jnp.zeros(packed, BF16))
            g = gp if g is None else g + gp
        rows = slice(il * PEER_NKEYS, (il + 1) * PEER_NKEYS)
        w_ref[rows, :] = g.reshape(PEER_NKEYS, tm) * _gelu(at[rows, :]).astype(BF16)
    acc_ref[...] += _dot(vt_ref[...], w_ref[...])

    @pl.when(e == pl.num_programs(2) - 1)
    def _():
        xo = x1_ref[0] + gt_ref[0] * acc_ref[...].T
        y_ref[0] = xo * lax.rsqrt(jnp.mean(xo * xo, axis=-1, keepdims=True) + EPS) * gf_ref[...]


def _peer(h2t, u, vt, cnt, e1, r2, e2, x1, mod3, gf, tm):
    b_ = h2t.shape[0]
    ne = PEER_EXPERTS // PEER_EC
    rows_per = PEER_EC // PEER_NKEYS * PEER_HEADS
    tok = lambda b, i, e: (b, i, 0)
    tokt = lambda b, i, e: (b, 0, i)
    return pl.pallas_call(
        _peer_body,
        grid=(b_, SEQ // tm, ne),
        in_specs=[pl.BlockSpec((1, D_MODEL, tm), tokt),
                  pl.BlockSpec((PEER_EC, D_MODEL), lambda b, i, e: (e, 0)),
                  pl.BlockSpec((D_MODEL, PEER_EC), lambda b, i, e: (0, e)),
                  pl.BlockSpec((1, rows_per, tm), lambda b, i, e: (b, e, i)),
                  pl.BlockSpec((1, rows_per, tm), lambda b, i, e: (b, e, i)),
                  pl.BlockSpec((1, PEER_QW, tm), tokt),
                  pl.BlockSpec((1, PEER_QW, tm), tokt),
                  pl.BlockSpec((1, tm, D_MODEL), tok),
                  pl.BlockSpec((1, 1, D_MODEL), lambda b, i, e: (b, 0, 5)),
                  pl.BlockSpec((1, D_MODEL), lambda b, i, e: (0, 0))],
        out_specs=pl.BlockSpec((1, tm, D_MODEL), tok),
        out_shape=jax.ShapeDtypeStruct((b_, SEQ, D_MODEL), F32),
        scratch_shapes=[pltpu.VMEM((D_MODEL, tm), F32), pltpu.VMEM((PEER_EC, tm), BF16)],
        compiler_params=_cparams("parallel", "parallel", "arbitrary"),
        name="peer",
    )(h2t, u, vt, cnt, e1, r2, e2, x1, mod3, gf)


def _peer_key_layouts(keys):
    ar = np.arange(PEER_HEADS)
    out = []
    for c in range(2):
        kc = keys[:, c].astype(BF16)
        m = jnp.zeros((PEER_NKEYS, PEER_HEADS, PEER_HEADS, PEER_HALF), BF16)
        m = m.at[:, ar, ar, :].set(kc.transpose(1, 0, 2))
        out.append(m.reshape(PEER_QW, PEER_QW))
    return out


def kernel(x, c, ctx, c_ctx, w_mod, b_mod, norm1, norm2, w_in, w_out, na_rpb, hg_lb, hg_norm,
           peer_wq, peer_keys, peer_u, peer_v, norm_f):
    assert w_mod.shape[0] == 1 and hg_lb.shape[0] == 2, "single-layer configuration"
    b_ = x.shape[0]
    pad = (-(b_ + 1)) % 8
    c_rows = jnp.concatenate([c, c_ctx[None], jnp.zeros((pad, D_MODEL), F32)], axis=0)
    mod = _mod(c_rows, w_mod[0], b_mod[0][None])
    mod3 = mod.reshape(mod.shape[0], 1, N_MOD * D_MODEL)

    w_in_b = w_in[0].astype(BF16)
    g1 = norm1[0][None]
    qkv_x, hg_x = _inproj(x, mod3, lambda b: b, g1, w_in_b, 512)
    qkv_c, hg_c = _inproj(ctx, mod3, lambda b: b_, g1, w_in_b, CTX_LEN)

    na = _na(qkv_x, qkv_c, _na_bias_table(na_rpb[0]))
    hg = _hgrn(hg_x, hg_c, hg_lb, hg_norm[0][None])

    wq = peer_wq[0].reshape(D_MODEL, PEER_HEADS, 2, PEER_HALF).transpose(0, 2, 1, 3)
    wq = wq.reshape(D_MODEL, 2 * PEER_QW).astype(BF16)
    k1, k2 = _peer_key_layouts(peer_keys[0])
    x1, h2t, s1, s2 = _mix(na, hg, x, mod3, norm2[0][None], w_out[0].astype(BF16), wq, k1, k2, 256)

    cnt, e1, r2, e2 = _route(s1, s2, 256)
    u = peer_u[0].astype(BF16)
    vt = peer_v[0].T.astype(BF16)
    return _peer(h2t, u, vt, cnt, e1, r2, e2, x1, mod3, norm_f[None], 512)
```
